```python
import math
import jax, jax.numpy as jnp
from jax import lax
import numpy as np

D_MODEL = 2048
BATCH = 4
SEQ = 4096
DEPTH = 4

CONV_DIM = 512
CONV_WIDTH = 3
SWA_HEADS = 8
SWA_KV_HEADS = 2
SWA_GROUP = SWA_HEADS // SWA_KV_HEADS
SWA_HEAD_DIM = 64
SWA_WINDOW = 128
SWA_BLOCK = 128
MLA_HEADS = 4
MLA_Q_RANK = 512
MLA_KV_RANK = 256
MLA_NOPE_DIM = 128
MLA_ROPE_DIM = 64
MLA_V_DIM = 128
MLA_QK_DIM = MLA_NOPE_DIM + MLA_ROPE_DIM
MLA_BLOCK = 128
ROPE_THETA = 10000.0
POOL_WINDOWS = (2, 4, 8, 16)
POOL_GROUPS = 4
POOL_GROUP_DIM = 128
POOL_DIM = POOL_GROUPS * POOL_GROUP_DIM
N_BRANCHES = 4
BRANCH_DIM = 512
IN_SIZES = (CONV_DIM, CONV_DIM, CONV_DIM,
            SWA_HEADS * SWA_HEAD_DIM, SWA_KV_HEADS * SWA_HEAD_DIM, SWA_KV_HEADS * SWA_HEAD_DIM,
            MLA_Q_RANK, MLA_KV_RANK, MLA_ROPE_DIM,
            POOL_DIM,
            N_BRANCHES * D_MODEL)
IN_DIM = sum(IN_SIZES)
PEER_HEADS = 8
PEER_N_KEYS = 128
PEER_N_EXPERTS = PEER_N_KEYS * PEER_N_KEYS
PEER_KEY_DIM = 128
PEER_TOPK = 16
PEER_CHUNK = 128
DN_ALPHA = (2 * DEPTH) ** 0.25
DN_BETA = (8 * DEPTH) ** -0.25
NORM_EPS = 1e-5
NEG_INF = -1e30

kernel_name = "hybrid_gated_conv_swa_mla_pool_peer_deepnorm"


def _layer_norm(x, g, b):
    x32 = x.astype(jnp.float32)
    mu = jnp.mean(x32, axis=-1, keepdims=True)
    var = jnp.mean(jnp.square(x32 - mu), axis=-1, keepdims=True)
    return ((x32 - mu) * lax.rsqrt(var + NORM_EPS)).astype(x.dtype) * g + b


def _rms_norm(x, g):
    x32 = x.astype(jnp.float32)
    return (x32 * lax.rsqrt(jnp.mean(x32 * x32, axis=-1, keepdims=True) + NORM_EPS)).astype(x.dtype) * g


def _split_columns(h):
    offsets = np.cumsum(np.array(IN_SIZES))[:-1].tolist()
    return jnp.split(h, offsets, axis=-1)


def _alibi_slopes(n):
    return jnp.asarray(2.0 ** (-8.0 * np.arange(1, n + 1) / n), dtype=jnp.float32)


def _rope(x, cos, sin):
    x1, x2 = jnp.split(x, 2, axis=-1)
    c = cos[:, None, :]
    s = sin[:, None, :]
    return jnp.concatenate([x1 * c - x2 * s, x2 * c + x1 * s], axis=-1)


def _short_conv(u, gate_b, gate_c, conv_w):
    z = gate_c * u
    y = lax.conv_general_dilated(z, conv_w[:, None, :].astype(z.dtype), window_strides=(1,),
                                 padding=[(CONV_WIDTH - 1, 0)],
                                 dimension_numbers=('NWC', 'WIO', 'NWC'),
                                 feature_group_count=CONV_DIM)
    return gate_b * y


def _sliding_window_attention(q, k, v, sinks):
    b, s, _ = q.shape
    nb = s // SWA_BLOCK
    qb = q.reshape(b, nb, SWA_BLOCK, SWA_KV_HEADS, SWA_GROUP, SWA_HEAD_DIM)
    kb = k.reshape(b, nb, SWA_BLOCK, SWA_KV_HEADS, SWA_HEAD_DIM)
    vb = v.reshape(b, nb, SWA_BLOCK, SWA_KV_HEADS, SWA_HEAD_DIM)
    pad = ((0, 0), (1, 0), (0, 0), (0, 0), (0, 0))
    k_band = jnp.concatenate([jnp.pad(kb, pad)[:, :-1], kb], axis=2)
    v_band = jnp.concatenate([jnp.pad(vb, pad)[:, :-1], vb], axis=2)
    sc = jnp.einsum('bnqkgd,bnskd->bnkgqs', qb, k_band).astype(jnp.float32) * (SWA_HEAD_DIM ** -0.5)
    i = jnp.arange(SWA_BLOCK)[:, None]
    j = jnp.arange(2 * SWA_BLOCK)[None, :]
    dist = i - j + SWA_BLOCK
    key_pos = jnp.arange(nb)[:, None] * SWA_BLOCK - SWA_BLOCK + jnp.arange(2 * SWA_BLOCK)[None, :]
    mask = ((dist >= 0) & (dist < SWA_WINDOW))[None] & (key_pos >= 0)[:, None, :]
    slopes = _alibi_slopes(SWA_HEADS).reshape(SWA_KV_HEADS, SWA_GROUP)
    sc = sc - slopes[:, :, None, None] * dist.astype(jnp.float32)
    sc = jnp.where(mask[None, :, None, None], sc, NEG_INF)
    sink = jnp.broadcast_to(sinks.astype(jnp.float32).reshape(1, 1, SWA_KV_HEADS, SWA_GROUP, 1, 1),
                            sc.shape[:-1] + (1,))
    probs = jax.nn.softmax(jnp.concatenate([sc, sink], axis=-1), axis=-1)[..., :-1]
    o = jnp.einsum('bnkgqs,bnskd->bnqkgd', probs.astype(v.dtype), v_band)
    return o.reshape(b, s, SWA_HEADS * SWA_HEAD_DIM)


def _latent_attention(c_q, c_kv, k_rope_in, q_norm, w_q_up, kv_norm, w_kv_up, cos, sin):
    b, s, _ = c_q.shape
    q = (_rms_norm(c_q, q_norm) @ w_q_up).reshape(b, s, MLA_HEADS, MLA_QK_DIM)
    q = jnp.concatenate([q[..., :MLA_NOPE_DIM], _rope(q[..., MLA_NOPE_DIM:], cos, sin)], axis=-1)
    kv = (_rms_norm(c_kv, kv_norm) @ w_kv_up).reshape(b, s, MLA_HEADS, MLA_NOPE_DIM + MLA_V_DIM)
    k_pe = _rope(k_rope_in[:, :, None, :], cos, sin)
    k = jnp.concatenate([kv[..., :MLA_NOPE_DIM],
                         jnp.broadcast_to(k_pe, (b, s, MLA_HEADS, MLA_ROPE_DIM))], axis=-1)
    v = kv[..., MLA_NOPE_DIM:]
    nb = s // MLA_BLOCK
    qb = q.reshape(b, nb, MLA_BLOCK, MLA_HEADS, MLA_QK_DIM).transpose(1, 0, 2, 3, 4)
    key_idx = jnp.arange(s)

    def block(args):
        q_blk, n = args
        sc = jnp.einsum('bqhd,bshd->bhqs', q_blk, k).astype(jnp.float32) * (MLA_QK_DIM ** -0.5)
        t = n * MLA_BLOCK + jnp.arange(MLA_BLOCK)
        sc = jnp.where(key_idx[None, :] <= t[:, None], sc, NEG_INF)
        p = jax.nn.softmax(sc, axis=-1).astype(v.dtype)
        return jnp.einsum('bhqs,bshd->bqhd', p, v)

    o = lax.map(block, (qb, jnp.arange(nb)))
    return o.transpose(1, 0, 2, 3, 4).reshape(b, s, MLA_HEADS * MLA_V_DIM)


def _multiscale_pool(u, pool_w, pool_scale):
    b, s, _ = u.shape
    u32 = u.astype(jnp.float32)
    cs0 = jnp.pad(jnp.cumsum(u32, axis=1), ((0, 0), (1, 0), (0, 0)))
    cur = cs0[:, 1:]
    pos1 = jnp.arange(s) + 1
    groups = []
    for g, w in enumerate(POOL_WINDOWS):
        sl = slice(g * POOL_GROUP_DIM, (g + 1) * POOL_GROUP_DIM)
        prev = jnp.pad(cs0[:, :, sl], ((0, 0), (w - 1, 0), (0, 0)))[:, :s]
        count = jnp.minimum(pos1, w).astype(jnp.float32)[None, :, None]
        groups.append((cur[:, :, sl] - prev) / count - u32[:, :, sl])
    y = jnp.stack(groups, axis=2).astype(u.dtype)
    y = jnp.einsum('bsgc,gcd->bsgd', y, pool_w).reshape(b, s, POOL_DIM)
    return y * pool_scale


def _token_mixer(x, w_in, conv_w, swa_sinks, mla_q_norm, mla_w_q_up, mla_kv_norm, mla_w_kv_up,
                 pool_w, pool_scale, w_branch, w_out, cos, sin):
    b, s, d = x.shape
    (conv_u, conv_b, conv_c, swa_q, swa_k, swa_v, mla_cq, mla_ckv, mla_kr,
     pool_u, gate_pre) = _split_columns(x @ w_in)
    branches = (
        _short_conv(conv_u, conv_b, conv_c, conv_w),
        _sliding_window_attention(swa_q, swa_k, swa_v, swa_sinks),
        _latent_attention(mla_cq, mla_ckv, mla_kr, mla_q_norm, mla_w_q_up, mla_kv_norm, mla_w_kv_up, cos, sin),
        _multiscale_pool(pool_u, pool_w, pool_scale),
    )
    gates = jax.nn.sigmoid(gate_pre).reshape(b, s, N_BRANCHES, d)
    merged = gates[:, :, 0] * (branches[0] @ w_branch[0])
    for i in range(1, N_BRANCHES):
        merged = merged + gates[:, :, i] * (branches[i] @ w_branch[i])
    return merged @ w_out


def _peer(x, w_query, sub_keys, u_table, v_table):
    b, s, d = x.shape
    xt = x.reshape(-1, d)
    t = xt.shape[0]
    q = (xt @ w_query).reshape(t, PEER_HEADS, 2, PEER_KEY_DIM)
    sc = jnp.einsum('thpk,pnk->thpn', q, sub_keys).astype(jnp.float32)
    top_s, top_i = lax.top_k(sc, PEER_TOPK)
    cand = (top_s[:, :, 0, :, None] + top_s[:, :, 1, None, :]).reshape(t, PEER_HEADS, PEER_TOPK * PEER_TOPK)
    best_s, best_j = lax.top_k(cand, PEER_TOPK)
    i1 = jnp.take_along_axis(top_i[:, :, 0], best_j // PEER_TOPK, axis=-1)
    i2 = jnp.take_along_axis(top_i[:, :, 1], best_j % PEER_TOPK, axis=-1)
    expert = i1 * PEER_N_KEYS + i2
    g = jax.nn.softmax(best_s, axis=-1).astype(x.dtype)
    nc = t // PEER_CHUNK
    hk = PEER_HEADS * PEER_TOPK

    def chunk(args):
        xc, idx, gc = args
        u = jnp.take(u_table, idx, axis=0)
        a = jnp.einsum('cd,ckd->ck', xc, u)
        coef = gc * jax.nn.gelu(a, approximate=False)
        vv = jnp.take(v_table, idx, axis=0)
        return jnp.einsum('ck,ckd->cd', coef, vv)

    out = lax.map(chunk, (xt.reshape(nc, PEER_CHUNK, d), expert.reshape(nc, PEER_CHUNK, hk),
                          g.reshape(nc, PEER_CHUNK, hk)))
    return out.reshape(b, s, d)


def setup_inputs(seed: int = 0) -> dict:
    key = jax.random.key(seed)
    ks = jax.random.split(key, 20)
    L, D = DEPTH, D_MODEL

    def nrm(k, shape, scale):
        return jax.random.normal(k, shape, jnp.float32) * scale

    return {
        "x": nrm(ks[0], (BATCH, SEQ, D), 1.0),
        "w_in": nrm(ks[1], (L, D, IN_DIM), D ** -0.5),
        "conv_w": nrm(ks[2], (L, CONV_WIDTH, CONV_DIM), CONV_WIDTH ** -0.5),
        "swa_sinks": nrm(ks[3], (L, SWA_HEADS), 0.5),
        "mla_q_norm": 1.0 + nrm(ks[4], (L, MLA_Q_RANK), 0.02),
        "mla_w_q_up": nrm(ks[5], (L, MLA_Q_RANK, MLA_HEADS * MLA_QK_DIM), MLA_Q_RANK ** -0.5),
        "mla_kv_norm": 1.0 + nrm(ks[6], (L, MLA_KV_RANK), 0.02),
        "mla_w_kv_up": nrm(ks[7], (L, MLA_KV_RANK, MLA_HEADS * (MLA_NOPE_DIM + MLA_V_DIM)), MLA_KV_RANK ** -0.5),
        "pool_w": nrm(ks[8], (L, POOL_GROUPS, POOL_GROUP_DIM, POOL_GROUP_DIM), POOL_GROUP_DIM ** -0.5),
        "pool_scale": 1.0 + nrm(ks[9], (L, POOL_DIM), 0.02),
        "w_branch": nrm(ks[10], (L, N_BRANCHES, BRANCH_DIM, D), DN_BETA * BRANCH_DIM ** -0.5),
        "w_out": nrm(ks[11], (L, D, D), DN_BETA * D ** -0.5),
        "ln1_g": 1.0 + nrm(ks[12], (L, D), 0.02),
        "ln1_b": nrm(ks[13], (L, D), 0.02),
        "peer_w_query": nrm(ks[14], (L, D, PEER_HEADS * 2 * PEER_KEY_DIM), D ** -0.5),
        "peer_sub_keys": nrm(ks[15], (L, 2, PEER_N_KEYS, PEER_KEY_DIM), PEER_KEY_DIM ** -0.5),
        "peer_u": nrm(ks[16], (L, PEER_N_EXPERTS, D), D ** -0.5),
        "peer_v": nrm(ks[17], (L, PEER_N_EXPERTS, D), DN_BETA * (PEER_HEADS * PEER_TOPK) ** -0.5),
        "ln2_g": 1.0 + nrm(ks[18], (L, D), 0.02),
        "ln2_b": nrm(ks[19], (L, D), 0.02),
    }


def reference(x, w_in, conv_w, swa_sinks, mla_q_norm, mla_w_q_up, mla_kv_norm, mla_w_kv_up,
              pool_w, pool_scale, w_branch, w_out, ln1_g, ln1_b, peer_w_query, peer_sub_keys,
              peer_u, peer_v, ln2_g, ln2_b):
    s = x.shape[1]
    pos = jnp.arange(s, dtype=jnp.float32)
    inv_freq = ROPE_THETA ** (-jnp.arange(0, MLA_ROPE_DIM, 2, dtype=jnp.float32) / MLA_ROPE_DIM)
    ang = pos[:, None] * inv_freq[None, :]
    cos = jnp.cos(ang).astype(x.dtype)
    sin = jnp.sin(ang).astype(x.dtype)
    for l in range(DEPTH):
        mix = _token_mixer(x, w_in[l], conv_w[l], swa_sinks[l], mla_q_norm[l], mla_w_q_up[l],
                           mla_kv_norm[l], mla_w_kv_up[l], pool_w[l], pool_scale[l], w_branch[l],
                           w_out[l], cos, sin)
        h = _layer_norm(DN_ALPHA * x + mix, ln1_g[l], ln1_b[l])
        ffn = _peer(h, peer_w_query[l], peer_sub_keys[l], peer_u[l], peer_v[l])
        x = _layer_norm(DN_ALPHA * h + ffn, ln2_g[l], ln2_b[l])
    return x
```

```python
import functools
import math

import numpy as np
import jax
import jax.numpy as jnp
from jax import lax
from jax.experimental import pallas as pl
from jax.experimental.pallas import tpu as pltpu

CONV_DIM = 512
SWA_HEADS = 8
SWA_KV_HEADS = 2
SWA_GROUP = SWA_HEADS // SWA_KV_HEADS
SWA_HEAD_DIM = 64
SWA_BLOCK = 128
MLA_HEADS = 4
MLA_Q_RANK = 512
MLA_KV_RANK = 256
MLA_NOPE_DIM = 128
MLA_ROPE_DIM = 64
MLA_V_DIM = 128
MLA_QK_DIM = MLA_NOPE_DIM + MLA_ROPE_DIM
ROPE_THETA = 10000.0
POOL_WINDOWS = (2, 4, 8, 16)
POOL_GROUPS = 4
POOL_GROUP_DIM = 128
N_BRANCHES = 4
BRANCH_DIM = 512
PEER_HEADS = 8
PEER_N_KEYS = 128
PEER_KEY_DIM = 128
PEER_TOPK = 16
DEPTH = 4
DN_ALPHA = (2 * DEPTH) ** 0.25
NORM_EPS = 1e-5
NEG_INF = -1e30

LANES = 128
MXU_DIM = 256
VMEM_LIMIT_BYTES = 56 * 1024 * 1024

MIX_CONV = 0
MIX_SWA_Q = 1536
MIX_MLA_CQ = 2048
MIX_POOL = 2560
MIX_MLA_CKV = 3072
MIX_SWA_K = 3328
MIX_SWA_V = 3456
MIX_KR = 3584
MIX_COLS = 3840

_SQRT_HALF = float(np.sqrt(0.5))


def _cparams(*sem):
    return pltpu.CompilerParams(dimension_semantics=sem, vmem_limit_bytes=VMEM_LIMIT_BYTES)


def _const_spec(shape):
    nd = len(shape)
    return pl.BlockSpec(shape, lambda *_: (0,) * nd, pipeline_mode=pl.Buffered(1))


def _rot_half_cols(w):
    half = MLA_ROPE_DIM // 2
    return jnp.concatenate([-w[:, half:], w[:, :half]], axis=1)


def _mm_kernel(x_ref, w_ref, o_ref):
    o_ref[...] = jnp.dot(x_ref[...], w_ref[...], preferred_element_type=jnp.float32).astype(o_ref.dtype)


def _matmul(x, w, tm, tn, out_dtype, name):
    m, k = x.shape
    n = w.shape[1]
    return pl.pallas_call(
        _mm_kernel,
        grid=(m // tm, n // tn),
        in_specs=[pl.BlockSpec((tm, k), lambda i, j: (i, 0)), pl.BlockSpec((k, tn), lambda i, j: (0, j))],
        out_specs=pl.BlockSpec((tm, tn), lambda i, j: (i, j)),
        out_shape=jax.ShapeDtypeStruct((m, n), out_dtype),
        compiler_params=_cparams("parallel", "parallel"),
        name=name,
    )(x, w)


def _convpool_kernel(c_ref, ch_ref, p_ref, ph_ref, cw_ref, pw_ref, ps_ref, oa_ref, od_ref, *, ts):
    i = pl.program_id(1)
    first = i == 0
    u = c_ref[:, 0:CONV_DIM]
    gb = c_ref[:, CONV_DIM:2 * CONV_DIM]
    gc = c_ref[:, 2 * CONV_DIM:3 * CONV_DIM]
    z = gc * u
    hz = ch_ref[:, 2 * CONV_DIM:3 * CONV_DIM] * ch_ref[:, 0:CONV_DIM]
    hz = jnp.where(first, 0.0, hz)
    row = lax.broadcasted_iota(jnp.int32, (ts, CONV_DIM), 0)
    z1 = jnp.where(row == 0, hz[7:8, :], pltpu.roll(z, 1, 0))
    z2 = jnp.where(row == 0, hz[6:7, :], jnp.where(row == 1, hz[7:8, :], pltpu.roll(z, 2, 0)))
    y = cw_ref[0:1, :] * z2 + cw_ref[1:2, :] * z1 + cw_ref[2:3, :] * z
    oa_ref[...] = (gb * y).astype(oa_ref.dtype)
    pu = p_ref[...]
    ph = jnp.where(first, 0.0, ph_ref[...])
    ext = jnp.concatenate([ph, pu], axis=0)
    sums = {1: ext}
    w = 1
    while w < POOL_WINDOWS[-1]:
        sums[2 * w] = sums[w] + pltpu.roll(sums[w], w, 0)
        w *= 2
    pos1 = (i * ts + lax.broadcasted_iota(jnp.int32, (ts, POOL_GROUP_DIM), 0) + 1).astype(jnp.float32)
    for g, w in enumerate(POOL_WINDOWS):
        sl = slice(g * POOL_GROUP_DIM, (g + 1) * POOL_GROUP_DIM)
        win = sums[w][16:, sl]
        cnt = jnp.minimum(pos1, float(w))
        yg = win / cnt - pu[:, sl]
        r = jnp.dot(yg.astype(jnp.bfloat16), pw_ref[g], preferred_element_type=jnp.float32)
        od_ref[:, sl] = (r * ps_ref[0:1, sl]).astype(od_ref.dtype)


def _convpool(hmix, conv_w, pool_w_bf, pool_scale, batch, seq, ts):
    t = hmix.shape[0]
    nst = seq // ts
    kern = functools.partial(_convpool_kernel, ts=ts)
    return pl.pallas_call(
        kern,
        grid=(batch, nst),
        in_specs=[
            pl.BlockSpec((ts, 3 * CONV_DIM), lambda b, i: (b * nst + i, 0)),
            pl.BlockSpec((8, 3 * CONV_DIM), lambda b, i: (jnp.maximum((b * nst + i) * (ts // 8) - 1, 0), 0)),
            pl.BlockSpec((ts, 512), lambda b, i: (b * nst + i, MIX_POOL // 512)),
            pl.BlockSpec((16, 512), lambda b, i: (jnp.maximum((b * nst + i) * (ts // 16) - 1, 0), MIX_POOL // 512)),
            _const_spec((8, CONV_DIM)),
            _const_spec((POOL_GROUPS, POOL_GROUP_DIM, POOL_GROUP_DIM)),
            _const_spec((8, 512)),
        ],
        out_specs=[pl.BlockSpec((ts, 512), lambda b, i: (b * nst + i, 0)),
                   pl.BlockSpec((ts, 512), lambda b, i: (b * nst + i, 0))],
        out_shape=[jax.ShapeDtypeStruct((t, 512), jnp.bfloat16), jax.ShapeDtypeStruct((t, 512), jnp.bfloat16)],
        compiler_params=_cparams("parallel", "parallel"),
        name="convpool",
    )(hmix, hmix, hmix, hmix, conv_w, pool_w_bf, pool_scale)


def _swa_kernel(sink_ref, q_ref, k_ref, kp_ref, v_ref, vp_ref, o_ref, *, rows):
    i = pl.program_id(1)
    nsub = rows // SWA_BLOCK
    kband = jnp.concatenate([kp_ref[...], k_ref[...]], axis=0)
    vband = jnp.concatenate([vp_ref[...], v_ref[...]], axis=0)
    lane = lax.broadcasted_iota(jnp.int32, kband.shape, 1)
    qi = lax.broadcasted_iota(jnp.int32, (SWA_BLOCK, 2 * SWA_BLOCK), 0)
    kj = lax.broadcasted_iota(jnp.int32, (SWA_BLOCK, 2 * SWA_BLOCK), 1)
    dist = qi - kj + SWA_BLOCK
    in_window = (dist >= 0) & (dist < SWA_BLOCK)
    distf = dist.astype(jnp.float32)
    scale = SWA_HEAD_DIM ** -0.5
    for kh in range(SWA_KV_HEADS):
        if kh == 0:
            k_lo = jnp.where(lane < SWA_HEAD_DIM, kband, 0.0)
            v_lo = jnp.where(lane < SWA_HEAD_DIM, vband, 0.0)
            k_hi = pltpu.roll(k_lo, SWA_HEAD_DIM, 1)
            v_hi = pltpu.roll(v_lo, SWA_HEAD_DIM, 1)
        else:
            k_hi = jnp.where(lane >= SWA_HEAD_DIM, kband, 0.0)
            v_hi = jnp.where(lane >= SWA_HEAD_DIM, vband, 0.0)
            k_lo = pltpu.roll(k_hi, SWA_HEAD_DIM, 1)
            v_lo = pltpu.roll(v_hi, SWA_HEAD_DIM, 1)
        k_lo, k_hi = k_lo.astype(jnp.bfloat16), k_hi.astype(jnp.bfloat16)
        v_lo, v_hi = v_lo.astype(jnp.bfloat16), v_hi.astype(jnp.bfloat16)
        for j in range(nsub):
            band = slice(j * SWA_BLOCK, (j + 2) * SWA_BLOCK)
            valid = in_window & ((i * nsub + j) * SWA_BLOCK - SWA_BLOCK + kj >= 0)
            for pair in range(SWA_GROUP // 2):
                col = (kh * SWA_GROUP + 2 * pair) * SWA_HEAD_DIM
                q2 = q_ref[j * SWA_BLOCK:(j + 1) * SWA_BLOCK, col:col + 2 * SWA_HEAD_DIM].astype(jnp.bfloat16)
                out = None
                for half, (kk, vv) in enumerate(((k_lo, v_lo), (k_hi, v_hi))):
                    h = kh * SWA_GROUP + 2 * pair + half
                    slope = 2.0 ** (-8.0 * (h + 1) / SWA_HEADS)
                    s = lax.dot_general(q2, kk[band], (((1,), (1,)), ((), ())),
                                        preferred_element_type=jnp.float32)
                    s = s * scale - slope * distf
                    s = jnp.where(valid, s, NEG_INF)
                    sink = sink_ref[h]
                    m = jnp.maximum(jnp.max(s, axis=-1, keepdims=True), sink)
                    e = jnp.exp(s - m)
                    den = jnp.sum(e, axis=-1, keepdims=True) + jnp.exp(sink - m)
                    p = (e / den).astype(jnp.bfloat16)
                    o = jnp.dot(p, vv[band], preferred_element_type=jnp.float32)
                    out = o if out is None else out + o
                o_ref[j * SWA_BLOCK:(j + 1) * SWA_BLOCK, col:col + 2 * SWA_HEAD_DIM] = out.astype(o_ref.dtype)


def _swa(hmix, sinks, batch, seq, rows):
    t = hmix.shape[0]
    nst = seq // rows
    rb = rows // SWA_BLOCK
    kern = functools.partial(_swa_kernel, rows=rows)
    prev = lambda b, i: jnp.maximum((b * nst + i) * rb - 1, 0)
    return pl.pallas_call(
        kern,
        grid=(batch, nst),
        in_specs=[
            pl.BlockSpec(memory_space=pltpu.SMEM),
            pl.BlockSpec((rows, 512), lambda b, i: (b * nst + i, MIX_SWA_Q // 512)),
            pl.BlockSpec((rows, LANES), lambda b, i: (b * nst + i, MIX_SWA_K // LANES)),
            pl.BlockSpec((SWA_BLOCK, LANES), lambda b, i: (prev(b, i), MIX_SWA_K // LANES)),
            pl.BlockSpec((rows, LANES), lambda b, i: (b * nst + i, MIX_SWA_V // LANES)),
            pl.BlockSpec((SWA_BLOCK, LANES), lambda b, i: (prev(b, i), MIX_SWA_V // LANES)),
        ],
        out_specs=pl.BlockSpec((rows, 512), lambda b, i: (b * nst + i, 0)),
        out_shape=jax.ShapeDtypeStruct((t, 512), jnp.bfloat16),
        compiler_params=_cparams("parallel", "parallel"),
        name="swa",
    )(sinks, hmix, hmix, hmix, hmix, hmix)


def _rms(x, g):
    return x * lax.rsqrt(jnp.mean(x * x, axis=-1, keepdims=True) + NORM_EPS) * g


def _mlaprep_kernel(cq_ref, ckv_ref, kr_ref, cs_ref, qn_ref, kn_ref, wq_ref, wkv_ref, q_ref, k_ref, v_ref):
    cs = cs_ref[...]
    lane = lax.broadcasted_iota(jnp.int32, cs.shape, 1)
    qn = _rms(cq_ref[...], qn_ref[0:1, :]).astype(jnp.bfloat16)
    qr = jnp.dot(qn, wq_ref[...], preferred_element_type=jnp.float32)
    kn = _rms(ckv_ref[...], kn_ref[0:1, :]).astype(jnp.bfloat16)
    kv = jnp.dot(kn, wkv_ref[...], preferred_element_type=jnp.float32)
    yk = kr_ref[...] * cs
    kpe = yk + pltpu.roll(yk, MLA_ROPE_DIM, 1)
    kpe = jnp.where(lane < MLA_ROPE_DIM, kpe, 0.0).astype(jnp.bfloat16)
    for h in range(MLA_HEADS):
        c0 = h * MXU_DIM
        q_ref[:, c0:c0 + LANES] = qr[:, c0:c0 + LANES].astype(jnp.bfloat16)
        yq = qr[:, c0 + LANES:c0 + 2 * LANES] * cs
        q_ref[:, c0 + LANES:c0 + 2 * LANES] = (yq + pltpu.roll(yq, MLA_ROPE_DIM, 1)).astype(jnp.bfloat16)
        k_ref[:, c0:c0 + LANES] = kv[:, h * LANES:(h + 1) * LANES].astype(jnp.bfloat16)
        k_ref[:, c0 + LANES:c0 + 2 * LANES] = kpe
    v_ref[...] = kv[:, MLA_HEADS * MLA_NOPE_DIM:].astype(jnp.bfloat16)


def _mlaprep(hmix, cs, qn, kn, wq, wkv, seq, tm):
    t = hmix.shape[0]
    nst = seq // tm
    return pl.pallas_call(
        _mlaprep_kernel,
        grid=(t // tm,),
        in_specs=[
            pl.BlockSpec((tm, 512), lambda i: (i, MIX_MLA_CQ // 512)),
            pl.BlockSpec((tm, 256), lambda i: (i, MIX_MLA_CKV // 256)),
            pl.BlockSpec((tm, LANES), lambda i: (i, MIX_KR // LANES)),
            pl.BlockSpec((tm, LANES), lambda i: (i % nst, 0)),
            _const_spec((8, MLA_Q_RANK)),
            _const_spec((8, MLA_KV_RANK)),
            _const_spec(wq.shape),
            _const_spec(wkv.shape),
        ],
        out_specs=[pl.BlockSpec((tm, MLA_HEADS * MXU_DIM), lambda i: (i, 0)),
                   pl.BlockSpec((tm, MLA_HEADS * MXU_DIM), lambda i: (i, 0)),
                   pl.BlockSpec((tm, MLA_HEADS * MLA_V_DIM), lambda i: (i, 0))],
        out_shape=[jax.ShapeDtypeStruct((t, MLA_HEADS * MXU_DIM), jnp.bfloat16),
                   jax.ShapeDtypeStruct((t, MLA_HEADS * MXU_DIM), jnp.bfloat16),
                   jax.ShapeDtypeStruct((t, MLA_HEADS * MLA_V_DIM), jnp.bfloat16)],
        compiler_params=_cparams("parallel"),
        name="mlaprep",
    )(hmix, hmix, hmix, cs, qn, kn, wq, wkv)


def _mla_kernel(q_ref, k_ref, v_ref, o_ref, *, tq):
    qi = pl.program_id(2)
    q = q_ref[...]
    scale = MLA_QK_DIM ** -0.5

    def step(j, carry, masked):
        m, l, acc = carry
        start = pl.multiple_of(j * tq, tq)
        k = k_ref[pl.ds(start, tq), :]
        v = v_ref[pl.ds(start, tq), :]
        s = lax.dot_general(q, k, (((1,), (1,)), ((), ())), preferred_element_type=jnp.float32) * scale
        if masked:
            r = lax.broadcasted_iota(jnp.int32, s.shape, 0)
            c = lax.broadcasted_iota(jnp.int32, s.shape, 1)
            s = jnp.where(c <= r, s, NEG_INF)
        m_new = jnp.maximum(m, jnp.max(s, axis=-1, keepdims=True))
        alpha = jnp.exp(m - m_new)
        p = jnp.exp(s - m_new)
        l = alpha * l + jnp.sum(p, axis=-1, keepdims=True)
        acc = alpha * acc + jnp.dot(p.astype(jnp.bfloat16), v, preferred_element_type=jnp.float32)
        return m_new, l, acc

    init = (jnp.full((tq, 1), NEG_INF, jnp.float32), jnp.zeros((tq, 1), jnp.float32),
            jnp.zeros((tq, MLA_V_DIM), jnp.float32))
    carry = lax.fori_loop(0, qi, lambda j, c: step(j, c, False), init)
    m, l, acc = step(qi, carry, True)
    o_ref[...] = (acc / l).astype(o_ref.dtype)


def _mla(qc, kc, vc, batch, seq, tq):
    t = qc.shape[0]
    nq = seq // tq
    kern = functools.partial(_mla_kernel, tq=tq)
    return pl.pallas_call(
        kern,
        grid=(batch, MLA_HEADS, nq),
        in_specs=[
            pl.BlockSpec((tq, MXU_DIM), lambda b, h, i: (b * nq + i, h)),
            pl.BlockSpec((seq, MXU_DIM), lambda b, h, i: (b, h)),
            pl.BlockSpec((seq, MLA_V_DIM), lambda b, h, i: (b, h)),
        ],
        out_specs=pl.BlockSpec((tq, MLA_V_DIM), lambda b, h, i: (b * nq + i, h)),
        out_shape=jax.ShapeDtypeStruct((t, MLA_HEADS * MLA_V_DIM), jnp.bfloat16),
        compiler_params=_cparams("parallel", "parallel", "arbitrary"),
        name="mla",
    )(qc, kc, vc)


def _layer_norm(y, g, b):
    mu = jnp.mean(y, axis=-1, keepdims=True)
    d = y - mu
    var = jnp.mean(d * d, axis=-1, keepdims=True)
    return d * lax.rsqrt(var + NORM_EPS) * g + b


def _merge_kernel(a_ref, b_ref, c_ref, d_ref, g_ref, x_ref, wb_ref, wo_ref, lg_ref, lb_ref, h_ref, hb_ref, *, dm):
    merged = None
    for i, br in enumerate((a_ref, b_ref, c_ref, d_ref)):
        y = jnp.dot(br[...], wb_ref[i], preferred_element_type=jnp.float32)
        gate = 1.0 / (1.0 + jnp.exp(-g_ref[:, i * dm:(i + 1) * dm]))
        merged = gate * y if merged is None else merged + gate * y
    mix = jnp.dot(merged.astype(jnp.bfloat16), wo_ref[...], preferred_element_type=jnp.float32)
    h = _layer_norm(DN_ALPHA * x_ref[...] + mix, lg_ref[0:1, :], lb_ref[0:1, :])
    h_ref[...] = h
    hb_ref[...] = h.astype(jnp.bfloat16)


def _merge(br, gpre, x, wb, wo, lg, lb, tm):
    t, dm = x.shape
    kern = functools.partial(_merge_kernel, dm=dm)
    row = lambda i: (i, 0)
    return pl.pallas_call(
        kern,
        grid=(t // tm,),
        in_specs=[pl.BlockSpec((tm, BRANCH_DIM), row)] * 4 + [
            pl.BlockSpec((tm, N_BRANCHES * dm), row),
            pl.BlockSpec((tm, dm), row),
            _const_spec(wb.shape),
            _const_spec(wo.shape),
            _const_spec((8, dm)),
            _const_spec((8, dm)),
        ],
        out_specs=[pl.BlockSpec((tm, dm), row), pl.BlockSpec((tm, dm), row)],
        out_shape=[jax.ShapeDtypeStruct((t, dm), jnp.float32), jax.ShapeDtypeStruct((t, dm), jnp.bfloat16)],
        compiler_params=_cparams("parallel"),
        name="merge",
    )(*br, gpre, x, wb, wo, lg, lb)


def _top16(x):
    rank = jnp.full(x.shape, float(PEER_TOPK), jnp.float32)
    vals = []
    for k in range(PEER_TOPK):
        m = jnp.max(x, axis=0, keepdims=True)
        eq = x == m
        rank = jnp.where(eq, float(k), rank)
        x = jnp.where(eq, -jnp.inf, x)
        vals.append(m)
    return jnp.concatenate(vals, axis=0), rank


def _peerq_kernel(hb_ref, wq_ref, sk_ref, r2_ref, e2_ref, nq_ref, d_ref, q_scr):
    q_scr[...] = jnp.dot(hb_ref[...], wq_ref[...], preferred_element_type=jnp.float32).astype(jnp.bfloat16)
    k = PEER_TOPK
    for h in range(PEER_HEADS):
        sc = []
        for p in range(2):
            c0 = (2 * h + p) * PEER_KEY_DIM
            sc.append(lax.dot_general(sk_ref[p], q_scr[:, c0:c0 + PEER_KEY_DIM], (((1,), (1,)), ((), ())),
                                      preferred_element_type=jnp.float32))
        a, r1 = _top16(sc[0])
        b, r2 = _top16(sc[1])
        blocks = [a[0:1] + b]
        qrow = lax.broadcasted_iota(jnp.int32, (8, a.shape[1]), 0)
        for p in range(1, 8):
            blocks.append(jnp.where(qrow < k // (p + 1), a[p:p + 1] + b[0:8], -jnp.inf))
        blocks.append(a[8:16] + b[0:1])
        cand = jnp.concatenate(blocks, axis=0)
        best, _ = _top16(cand)
        top = best[0:1]
        sel = cand >= best[k - 1:k]
        z = jnp.sum(jnp.where(sel, jnp.exp(cand - top), 0.0), axis=0, keepdims=True)
        self32 = sel.astype(jnp.float32)
        n_rows = [jnp.sum(self32[0:16], axis=0, keepdims=True)]
        for p in range(1, 8):
            n_rows.append(jnp.sum(self32[8 + 8 * p:16 + 8 * p], axis=0, keepdims=True))
        nq = jnp.zeros(r1.shape, jnp.float32)
        for p in range(8):
            nq = jnp.where(r1 == float(p), n_rows[p], nq)
        tail = self32[72:80]
        for p in range(8, 16):
            nq = jnp.where(r1 == float(p), tail[p - 8:p - 7], nq)
        nq_ref[h] = nq
        d_ref[h] = jnp.exp(sc[0] - a[0:1]) / z
        r2_ref[h] = r2
        e2_ref[h] = jnp.exp(sc[1] - b[0:1])


def _peerq(hb, wq, sk, tm):
    t, dm = hb.shape
    shp = jax.ShapeDtypeStruct((PEER_HEADS, PEER_N_KEYS, t), jnp.float32)
    ospec = pl.BlockSpec((PEER_HEADS, PEER_N_KEYS, tm), lambda i: (0, 0, i))
    return pl.pallas_call(
        _peerq_kernel,
        grid=(t // tm,),
        in_specs=[pl.BlockSpec((tm, dm), lambda i: (i, 0)), _const_spec(wq.shape), _const_spec(sk.shape)],
        out_specs=[ospec] * 4,
        out_shape=[shp] * 4,
        scratch_shapes=[pltpu.VMEM((tm, wq.shape[1]), jnp.bfloat16)],
        compiler_params=_cparams("parallel"),
        name="peerq",
    )(hb, wq, sk)


def _peerd_kernel(hb_ref, u_ref, v_ref, r2_ref, e2_ref, nq_ref, d_ref, o_ref, coef_scr):
    j = pl.program_id(1)

    @pl.when(j == 0)
    def _():
        o_ref[...] = jnp.zeros_like(o_ref)

    at = lax.dot_general(u_ref[...], hb_ref[...], (((1,), (1,)), ((), ())),
                         preferred_element_type=jnp.float32)
    n1 = u_ref.shape[0] // PEER_N_KEYS
    for r in range(n1):
        rows = slice(r * PEER_N_KEYS, (r + 1) * PEER_N_KEYS)
        w = None
        for h in range(PEER_HEADS):
            term = jnp.where(r2_ref[h] < nq_ref[h, r:r + 1, :], e2_ref[h], 0.0) * d_ref[h, r:r + 1, :]
            w = term if w is None else w + term
        a = at[rows]
        gelu = 0.5 * a * (1.0 + lax.erf(a * _SQRT_HALF))
        coef_scr[rows, :] = (w * gelu).astype(jnp.bfloat16)
    o_ref[...] += lax.dot_general(coef_scr[...], v_ref[...], (((0,), (0,)), ((), ())),
                                  preferred_element_type=jnp.float32)


def _peerd(hb, ub, vb, r2, e2, nq, dd, tm, te):
    t, dm = hb.shape
    ne = ub.shape[0]
    n1 = te // PEER_N_KEYS
    full = pl.BlockSpec((PEER_HEADS, PEER_N_KEYS, tm), lambda i, j: (0, 0, i))
    part = pl.BlockSpec((PEER_HEADS, n1, tm), lambda i, j: (0, j, i))
    return pl.pallas_call(
        _peerd_kernel,
        grid=(t // tm, ne // te),
        in_specs=[
            pl.BlockSpec((tm, dm), lambda i, j: (i, 0)),
            pl.BlockSpec((te, dm), lambda i, j: (j, 0)),
            pl.BlockSpec((te, dm), lambda i, j: (j, 0)),
            full, full, part, part,
        ],
        out_specs=pl.BlockSpec((tm, dm), lambda i, j: (i, 0)),
        out_shape=jax.ShapeDtypeStruct((t, dm), jnp.float32),
        scratch_shapes=[pltpu.VMEM((te, tm), jnp.bfloat16)],
        compiler_params=_cparams("parallel", "arbitrary"),
        name="peerd",
    )(hb, ub, vb, r2, e2, nq, dd)


def _ln2_kernel(h_ref, f_ref, g_ref, b_ref, x_ref, xb_ref):
    x = _layer_norm(DN_ALPHA * h_ref[...] + f_ref[...], g_ref[0:1, :], b_ref[0:1, :])
    x_ref[...] = x
    xb_ref[...] = x.astype(jnp.bfloat16)


def _ln2(h, ffn, g, b, tm):
    t, dm = h.shape
    row = lambda i: (i, 0)
    return pl.pallas_call(
        _ln2_kernel,
        grid=(t // tm,),
        in_specs=[pl.BlockSpec((tm, dm), row), pl.BlockSpec((tm, dm), row), _const_spec((8, dm)), _const_spec((8, dm))],
        out_specs=[pl.BlockSpec((tm, dm), row), pl.BlockSpec((tm, dm), row)],
        out_shape=[jax.ShapeDtypeStruct((t, dm), jnp.float32), jax.ShapeDtypeStruct((t, dm), jnp.bfloat16)],
        compiler_params=_cparams("parallel"),
        name="ln2",
    )(h, ffn, g, b)


def _tiles(t, seq):
    pick = lambda n, want: want if n % want == 0 else n
    return dict(
        mm_m=pick(t, 1024), mix_n=1280, gate_n=1024,
        convpool=pick(seq, 512), swa=pick(seq, 512), mlaprep=pick(seq, 512), mla_q=pick(seq, 512),
        merge=pick(t, 256), peerq=pick(t, 256), peerd_m=pick(t, 512), peerd_e=1024, ln2=pick(t, 512),
    )


def _row8(v):
    return jnp.broadcast_to(v.reshape(1, -1), (8, v.shape[-1]))


def kernel(x, w_in, conv_w, swa_sinks, mla_q_norm, mla_w_q_up, mla_kv_norm, mla_w_kv_up, pool_w, pool_scale,
           w_branch, w_out, ln1_g, ln1_b, peer_w_query, peer_sub_keys, peer_u, peer_v, ln2_g, ln2_b):
    batch, seq, dm = x.shape
    t = batch * seq
    depth = w_in.shape[0]
    tl = _tiles(t, seq)
    bf = jnp.bfloat16

    pos = jnp.arange(seq, dtype=jnp.float32)
    inv_freq = ROPE_THETA ** (-jnp.arange(0, MLA_ROPE_DIM, 2, dtype=jnp.float32) / MLA_ROPE_DIM)
    ang = pos[:, None] * inv_freq[None, :]
    cos, sin = jnp.cos(ang), jnp.sin(ang)
    cs = jnp.concatenate([cos, cos, sin, sin], axis=1)

    o_u, o_b, o_c, o_sq, o_sk, o_sv, o_cq, o_ckv, o_kr, o_pool, o_gate = np.cumsum(
        [0, 512, 512, 512, 512, 128, 128, 512, 256, 64, 512]).tolist()

    xf = x.reshape(t, dm)
    xb = xf.astype(bf)
    for l in range(depth):
        wi = w_in[l]
        w_kr = wi[:, o_kr:o_kr + MLA_ROPE_DIM]
        w_mix = jnp.concatenate([
            wi[:, o_u:o_sq],
            wi[:, o_sq:o_sq + 512],
            wi[:, o_cq:o_cq + 512],
            wi[:, o_pool:o_pool + 512],
            wi[:, o_ckv:o_ckv + 256],
            wi[:, o_sk:o_sk + 128],
            wi[:, o_sv:o_sv + 128],
            w_kr, _rot_half_cols(w_kr),
            jnp.zeros((dm, MIX_COLS - MIX_KR - 2 * MLA_ROPE_DIM), wi.dtype),
        ], axis=1).astype(bf)
        w_gate = wi[:, o_gate:].astype(bf)

        wq_up = mla_w_q_up[l].reshape(MLA_Q_RANK, MLA_HEADS, MLA_QK_DIM)
        wq_parts = []
        for h in range(MLA_HEADS):
            w_r = wq_up[:, h, MLA_NOPE_DIM:]
            wq_parts += [wq_up[:, h, :MLA_NOPE_DIM], w_r, _rot_half_cols(w_r)]
        wq = jnp.concatenate(wq_parts, axis=1).astype(bf)
        wkv_up = mla_w_kv_up[l].reshape(MLA_KV_RANK, MLA_HEADS, MLA_NOPE_DIM + MLA_V_DIM)
        wkv = jnp.concatenate([wkv_up[:, :, :MLA_NOPE_DIM].reshape(MLA_KV_RANK, -1),
                               wkv_up[:, :, MLA_NOPE_DIM:].reshape(MLA_KV_RANK, -1)], axis=1).astype(bf)

        hmix = _matmul(xb, w_mix, tl["mm_m"], tl["mix_n"], jnp.float32, "inproj_mix")
        gpre = _matmul(xb, w_gate, tl["mm_m"], tl["gate_n"], jnp.float32, "inproj_gate")

        br_a, br_d = _convpool(hmix, jnp.pad(conv_w[l], ((0, 5), (0, 0))), pool_w[l].astype(bf),
                               _row8(pool_scale[l]), batch, seq, tl["convpool"])
        br_b = _swa(hmix, swa_sinks[l], batch, seq, tl["swa"])
        qc, kc, vc = _mlaprep(hmix, cs, _row8(mla_q_norm[l]), _row8(mla_kv_norm[l]), wq, wkv, seq, tl["mlaprep"])
        br_c = _mla(qc, kc, vc, batch, seq, tl["mla_q"])

        h, hb = _merge((br_a, br_b, br_c, br_d), gpre, xf, w_branch[l].astype(bf), w_out[l].astype(bf),
                       _row8(ln1_g[l]), _row8(ln1_b[l]), tl["merge"])

        r2, e2, nq, dd = _peerq(hb, peer_w_query[l].astype(bf), peer_sub_keys[l].astype(bf), tl["peerq"])
        ffn = _peerd(hb, peer_u[l].astype(bf), peer_v[l].astype(bf), r2, e2, nq, dd, tl["peerd_m"], tl["peerd_e"])
        xf, xb = _ln2(h, ffn, _row8(ln2_g[l]), _row8(ln2_b[l]), tl["ln2"])
    return xf.reshape(batch, seq, dm)
```

```python
import functools
import math

import numpy as np
import jax
import jax.numpy as jnp
from jax import lax
from jax.experimental import pallas as pl
from jax.experimental.pallas import tpu as pltpu

CONV_DIM = 512
SWA_HEADS = 8
SWA_KV_HEADS = 2
SWA_GROUP = SWA_HEADS // SWA_KV_HEADS
SWA_HEAD_DIM = 64
SWA_BLOCK = 128
MLA_HEADS = 4
MLA_Q_RANK = 512
MLA_KV_RANK = 256
MLA_NOPE_DIM = 128
MLA_ROPE_DIM = 64
MLA_V_DIM = 128
MLA_QK_DIM = MLA_NOPE_DIM + MLA_ROPE_DIM
ROPE_THETA = 10000.0
POOL_WINDOWS = (2, 4, 8, 16)
POOL_GROUPS = 4
POOL_GROUP_DIM = 128
N_BRANCHES = 4
BRANCH_DIM = 512
PEER_HEADS = 8
PEER_N_KEYS = 128
PEER_KEY_DIM = 128
PEER_TOPK = 16
DEPTH = 4
DN_ALPHA = (2 * DEPTH) ** 0.25
NORM_EPS = 1e-5
NEG_INF = -1e30

LANES = 128
MXU_DIM = 256
VMEM_LIMIT_BYTES = 56 * 1024 * 1024

MIX_CONV = 0
MIX_SWA_Q = 1536
MIX_MLA_CQ = 2048
MIX_POOL = 2560
MIX_MLA_CKV = 3072
MIX_SWA_K = 3328
MIX_SWA_V = 3456
MIX_KR = 3584
MIX_COLS = 3840

_SQRT_HALF = float(np.sqrt(0.5))


def _cparams(*sem, flags=None):
    return pltpu.CompilerParams(dimension_semantics=sem, vmem_limit_bytes=VMEM_LIMIT_BYTES, flags=flags)


def _const_spec(shape):
    nd = len(shape)
    return pl.BlockSpec(shape, lambda *_: (0,) * nd, pipeline_mode=pl.Buffered(1))


def _rot_half_cols(w):
    half = MLA_ROPE_DIM // 2
    return jnp.concatenate([-w[:, half:], w[:, :half]], axis=1)


def _mm_kernel(x_ref, w_ref, o_ref):
    o_ref[...] = jnp.dot(x_ref[...], w_ref[...], preferred_element_type=jnp.float32).astype(o_ref.dtype)


def _matmul(x, w, tm, tn, out_dtype, name):
    m, k = x.shape
    n = w.shape[1]
    return pl.pallas_call(
        _mm_kernel,
        grid=(m // tm, n // tn),
        in_specs=[pl.BlockSpec((tm, k), lambda i, j: (i, 0)), pl.BlockSpec((k, tn), lambda i, j: (0, j))],
        out_specs=pl.BlockSpec((tm, tn), lambda i, j: (i, j)),
        out_shape=jax.ShapeDtypeStruct((m, n), out_dtype),
        compiler_params=_cparams("parallel", "parallel"),
        name=name,
    )(x, w)


def _convpool_kernel(c_ref, ch_ref, p_ref, ph_ref, cw_ref, pw_ref, ps_ref, oa_ref, od_ref, *, ts):
    i = pl.program_id(1)
    first = i == 0
    u = c_ref[:, 0:CONV_DIM]
    gb = c_ref[:, CONV_DIM:2 * CONV_DIM]
    gc = c_ref[:, 2 * CONV_DIM:3 * CONV_DIM]
    z = gc * u
    hz = ch_ref[:, 2 * CONV_DIM:3 * CONV_DIM] * ch_ref[:, 0:CONV_DIM]
    hz = jnp.where(first, 0.0, hz)
    row = lax.broadcasted_iota(jnp.int32, (ts, CONV_DIM), 0)
    z1 = jnp.where(row == 0, hz[7:8, :], pltpu.roll(z, 1, 0))
    z2 = jnp.where(row == 0, hz[6:7, :], jnp.where(row == 1, hz[7:8, :], pltpu.roll(z, 2, 0)))
    y = cw_ref[0:1, :] * z2 + cw_ref[1:2, :] * z1 + cw_ref[2:3, :] * z
    oa_ref[...] = (gb * y).astype(oa_ref.dtype)
    pu = p_ref[...]
    ph = jnp.where(first, 0.0, ph_ref[...])
    ext = jnp.concatenate([ph, pu], axis=0)
    sums = {1: ext}
    w = 1
    while w < POOL_WINDOWS[-1]:
        sums[2 * w] = sums[w] + pltpu.roll(sums[w], w, 0)
        w *= 2
    pos1 = (i * ts + lax.broadcasted_iota(jnp.int32, (ts, POOL_GROUP_DIM), 0) + 1).astype(jnp.float32)
    for g, w in enumerate(POOL_WINDOWS):
        sl = slice(g * POOL_GROUP_DIM, (g + 1) * POOL_GROUP_DIM)
        win = sums[w][16:, sl]
        cnt = jnp.minimum(pos1, float(w))
        yg = win / cnt - pu[:, sl]
        r = jnp.dot(yg.astype(jnp.bfloat16), pw_ref[g], preferred_element_type=jnp.float32)
        od_ref[:, sl] = (r * ps_ref[0:1, sl]).astype(od_ref.dtype)


def _convpool(hmix, conv_w, pool_w_bf, pool_scale, batch, seq, ts):
    t = hmix.shape[0]
    nst = seq // ts
    kern = functools.partial(_convpool_kernel, ts=ts)
    return pl.pallas_call(
        kern,
        grid=(batch, nst),
        in_specs=[
            pl.BlockSpec((ts, 3 * CONV_DIM), lambda b, i: (b * nst + i, 0)),
            pl.BlockSpec((8, 3 * CONV_DIM), lambda b, i: (jnp.maximum((b * nst + i) * (ts // 8) - 1, 0), 0)),
            pl.BlockSpec((ts, 512), lambda b, i: (b * nst + i, MIX_POOL // 512)),
            pl.BlockSpec((16, 512), lambda b, i: (jnp.maximum((b * nst + i) * (ts // 16) - 1, 0), MIX_POOL // 512)),
            _const_spec((8, CONV_DIM)),
            _const_spec((POOL_GROUPS, POOL_GROUP_DIM, POOL_GROUP_DIM)),
            _const_spec((8, 512)),
        ],
        out_specs=[pl.BlockSpec((ts, 512), lambda b, i: (b * nst + i, 0)),
                   pl.BlockSpec((ts, 512), lambda b, i: (b * nst + i, 0))],
        out_shape=[jax.ShapeDtypeStruct((t, 512), jnp.bfloat16), jax.ShapeDtypeStruct((t, 512), jnp.bfloat16)],
        compiler_params=_cparams("parallel", "parallel"),
        name="convpool",
    )(hmix, hmix, hmix, hmix, conv_w, pool_w_bf, pool_scale)


def _swa_kernel(sink_ref, q_ref, k_ref, kp_ref, v_ref, vp_ref, o_ref, *, rows):
    i = pl.program_id(1)
    nsub = rows // SWA_BLOCK
    kband = jnp.concatenate([kp_ref[...], k_ref[...]], axis=0)
    vband = jnp.concatenate([vp_ref[...], v_ref[...]], axis=0)
    lane = lax.broadcasted_iota(jnp.int32, kband.shape, 1)
    qi = lax.broadcasted_iota(jnp.int32, (SWA_BLOCK, 2 * SWA_BLOCK), 0)
    kj = lax.broadcasted_iota(jnp.int32, (SWA_BLOCK, 2 * SWA_BLOCK), 1)
    dist = qi - kj + SWA_BLOCK
    in_window = (dist >= 0) & (dist < SWA_BLOCK)
    distf = dist.astype(jnp.float32)
    scale = SWA_HEAD_DIM ** -0.5
    for kh in range(SWA_KV_HEADS):
        if kh == 0:
            k_lo = jnp.where(lane < SWA_HEAD_DIM, kband, 0.0)
            v_lo = jnp.where(lane < SWA_HEAD_DIM, vband, 0.0)
            k_hi = pltpu.roll(k_lo, SWA_HEAD_DIM, 1)
            v_hi = pltpu.roll(v_lo, SWA_HEAD_DIM, 1)
        else:
            k_hi = jnp.where(lane >= SWA_HEAD_DIM, kband, 0.0)
            v_hi = jnp.where(lane >= SWA_HEAD_DIM, vband, 0.0)
            k_lo = pltpu.roll(k_hi, SWA_HEAD_DIM, 1)
            v_lo = pltpu.roll(v_hi, SWA_HEAD_DIM, 1)
        k_lo, k_hi = k_lo.astype(jnp.bfloat16), k_hi.astype(jnp.bfloat16)
        v_lo, v_hi = v_lo.astype(jnp.bfloat16), v_hi.astype(jnp.bfloat16)
        for j in range(nsub):
            band = slice(j * SWA_BLOCK, (j + 2) * SWA_BLOCK)
            valid = in_window & ((i * nsub + j) * SWA_BLOCK - SWA_BLOCK + kj >= 0)
            for pair in range(SWA_GROUP // 2):
                col = (kh * SWA_GROUP + 2 * pair) * SWA_HEAD_DIM
                q2 = q_ref[j * SWA_BLOCK:(j + 1) * SWA_BLOCK, col:col + 2 * SWA_HEAD_DIM].astype(jnp.bfloat16)
                out = None
                for half, (kk, vv) in enumerate(((k_lo, v_lo), (k_hi, v_hi))):
                    h = kh * SWA_GROUP + 2 * pair + half
                    slope = 2.0 ** (-8.0 * (h + 1) / SWA_HEADS)
                    s = lax.dot_general(q2, kk[band], (((1,), (1,)), ((), ())),
                                        preferred_element_type=jnp.float32)
                    s = s * scale - slope * distf
                    s = jnp.where(valid, s, NEG_INF)
                    sink = sink_ref[h]
                    m = jnp.maximum(jnp.max(s, axis=-1, keepdims=True), sink)
                    e = jnp.exp(s - m)
                    den = jnp.sum(e, axis=-1, keepdims=True) + jnp.exp(sink - m)
                    p = (e / den).astype(jnp.bfloat16)
                    o = jnp.dot(p, vv[band], preferred_element_type=jnp.float32)
                    out = o if out is None else out + o
                o_ref[j * SWA_BLOCK:(j + 1) * SWA_BLOCK, col:col + 2 * SWA_HEAD_DIM] = out.astype(o_ref.dtype)


def _swa(hmix, sinks, batch, seq, rows):
    t = hmix.shape[0]
    nst = seq // rows
    rb = rows // SWA_BLOCK
    kern = functools.partial(_swa_kernel, rows=rows)
    prev = lambda b, i: jnp.maximum((b * nst + i) * rb - 1, 0)
    return pl.pallas_call(
        kern,
        grid=(batch, nst),
        in_specs=[
            pl.BlockSpec(memory_space=pltpu.SMEM),
            pl.BlockSpec((rows, 512), lambda b, i: (b * nst + i, MIX_SWA_Q // 512)),
            pl.BlockSpec((rows, LANES), lambda b, i: (b * nst + i, MIX_SWA_K // LANES)),
            pl.BlockSpec((SWA_BLOCK, LANES), lambda b, i: (prev(b, i), MIX_SWA_K // LANES)),
            pl.BlockSpec((rows, LANES), lambda b, i: (b * nst + i, MIX_SWA_V // LANES)),
            pl.BlockSpec((SWA_BLOCK, LANES), lambda b, i: (prev(b, i), MIX_SWA_V // LANES)),
        ],
        out_specs=pl.BlockSpec((rows, 512), lambda b, i: (b * nst + i, 0)),
        out_shape=jax.ShapeDtypeStruct((t, 512), jnp.bfloat16),
        compiler_params=_cparams("parallel", "parallel"),
        name="swa",
    )(sinks, hmix, hmix, hmix, hmix, hmix)


def _rms(x, g):
    return x * lax.rsqrt(jnp.mean(x * x, axis=-1, keepdims=True) + NORM_EPS) * g


def _mlaprep_kernel(cq_ref, ckv_ref, kr_ref, cs_ref, qn_ref, kn_ref, wq_ref, wkv_ref, q_ref, k_ref, v_ref):
    cs = cs_ref[...]
    lane = lax.broadcasted_iota(jnp.int32, cs.shape, 1)
    qn = _rms(cq_ref[...], qn_ref[0:1, :]).astype(jnp.bfloat16)
    qr = jnp.dot(qn, wq_ref[...], preferred_element_type=jnp.float32)
    kn = _rms(ckv_ref[...], kn_ref[0:1, :]).astype(jnp.bfloat16)
    kv = jnp.dot(kn, wkv_ref[...], preferred_element_type=jnp.float32)
    yk = kr_ref[...] * cs
    kpe = yk + pltpu.roll(yk, MLA_ROPE_DIM, 1)
    kpe = jnp.where(lane < MLA_ROPE_DIM, kpe, 0.0).astype(jnp.bfloat16)
    for h in range(MLA_HEADS):
        c0 = h * MXU_DIM
        q_ref[:, c0:c0 + LANES] = qr[:, c0:c0 + LANES].astype(jnp.bfloat16)
        yq = qr[:, c0 + LANES:c0 + 2 * LANES] * cs
        q_ref[:, c0 + LANES:c0 + 2 * LANES] = (yq + pltpu.roll(yq, MLA_ROPE_DIM, 1)).astype(jnp.bfloat16)
        k_ref[:, c0:c0 + LANES] = kv[:, h * LANES:(h + 1) * LANES].astype(jnp.bfloat16)
        k_ref[:, c0 + LANES:c0 + 2 * LANES] = kpe
    v_ref[...] = kv[:, MLA_HEADS * MLA_NOPE_DIM:].astype(jnp.bfloat16)


def _mlaprep(hmix, cs, qn, kn, wq, wkv, seq, tm):
    t = hmix.shape[0]
    nst = seq // tm
    return pl.pallas_call(
        _mlaprep_kernel,
        grid=(t // tm,),
        in_specs=[
            pl.BlockSpec((tm, 512), lambda i: (i, MIX_MLA_CQ // 512)),
            pl.BlockSpec((tm, 256), lambda i: (i, MIX_MLA_CKV // 256)),
            pl.BlockSpec((tm, LANES), lambda i: (i, MIX_KR // LANES)),
            pl.BlockSpec((tm, LANES), lambda i: (i % nst, 0)),
            _const_spec((8, MLA_Q_RANK)),
            _const_spec((8, MLA_KV_RANK)),
            _const_spec(wq.shape),
            _const_spec(wkv.shape),
        ],
        out_specs=[pl.BlockSpec((tm, MLA_HEADS * MXU_DIM), lambda i: (i, 0)),
                   pl.BlockSpec((tm, MLA_HEADS * MXU_DIM), lambda i: (i, 0)),
                   pl.BlockSpec((tm, MLA_HEADS * MLA_V_DIM), lambda i: (i, 0))],
        out_shape=[jax.ShapeDtypeStruct((t, MLA_HEADS * MXU_DIM), jnp.bfloat16),
                   jax.ShapeDtypeStruct((t, MLA_HEADS * MXU_DIM), jnp.bfloat16),
                   jax.ShapeDtypeStruct((t, MLA_HEADS * MLA_V_DIM), jnp.bfloat16)],
        compiler_params=_cparams("parallel"),
        name="mlaprep",
    )(hmix, hmix, hmix, cs, qn, kn, wq, wkv)


def _mla_kernel(q_ref, k_ref, v_ref, o_ref, *, tq):
    qi = pl.program_id(2)
    q = q_ref[...]
    scale = MLA_QK_DIM ** -0.5

    def step(j, carry, masked):
        m, l, acc = carry
        start = pl.multiple_of(j * tq, tq)
        k = k_ref[pl.ds(start, tq), :]
        v = v_ref[pl.ds(start, tq), :]
        s = lax.dot_general(q, k, (((1,), (1,)), ((), ())), preferred_element_type=jnp.float32) * scale
        if masked:
            r = lax.broadcasted_iota(jnp.int32, s.shape, 0)
            c = lax.broadcasted_iota(jnp.int32, s.shape, 1)
            s = jnp.where(c <= r, s, NEG_INF)
        m_new = jnp.maximum(m, jnp.max(s, axis=-1, keepdims=True))
        alpha = jnp.exp(m - m_new)
        p = jnp.exp(s - m_new)
        l = alpha * l + jnp.sum(p, axis=-1, keepdims=True)
        acc = alpha * acc + jnp.dot(p.astype(jnp.bfloat16), v, preferred_element_type=jnp.float32)
        return m_new, l, acc

    init = (jnp.full((tq, 1), NEG_INF, jnp.float32), jnp.zeros((tq, 1), jnp.float32),
            jnp.zeros((tq, MLA_V_DIM), jnp.float32))
    carry = lax.fori_loop(0, qi, lambda j, c: step(j, c, False), init)
    m, l, acc = step(qi, carry, True)
    o_ref[...] = (acc / l).astype(o_ref.dtype)


def _mla(qc, kc, vc, batch, seq, tq):
    t = qc.shape[0]
    nq = seq // tq
    kern = functools.partial(_mla_kernel, tq=tq)
    return pl.pallas_call(
        kern,
        grid=(batch, MLA_HEADS, nq),
        in_specs=[
            pl.BlockSpec((tq, MXU_DIM), lambda b, h, i: (b * nq + i, h)),
            pl.BlockSpec((seq, MXU_DIM), lambda b, h, i: (b, h)),
            pl.BlockSpec((seq, MLA_V_DIM), lambda b, h, i: (b, h)),
        ],
        out_specs=pl.BlockSpec((tq, MLA_V_DIM), lambda b, h, i: (b * nq + i, h)),
        out_shape=jax.ShapeDtypeStruct((t, MLA_HEADS * MLA_V_DIM), jnp.bfloat16),
        compiler_params=_cparams("parallel", "parallel", "arbitrary"),
        name="mla",
    )(qc, kc, vc)


def _layer_norm(y, g, b):
    mu = jnp.mean(y, axis=-1, keepdims=True)
    d = y - mu
    var = jnp.mean(d * d, axis=-1, keepdims=True)
    return d * lax.rsqrt(var + NORM_EPS) * g + b


def _merge_kernel(a_ref, b_ref, c_ref, d_ref, g_ref, x_ref, wb_ref, wo_ref, lg_ref, lb_ref, h_ref, hb_ref, *, dm):
    merged = None
    for i, br in enumerate((a_ref, b_ref, c_ref, d_ref)):
        y = jnp.dot(br[...], wb_ref[i], preferred_element_type=jnp.float32)
        gate = 1.0 / (1.0 + jnp.exp(-g_ref[:, i * dm:(i + 1) * dm]))
        merged = gate * y if merged is None else merged + gate * y
    mix = jnp.dot(merged.astype(jnp.bfloat16), wo_ref[...], preferred_element_type=jnp.float32)
    h = _layer_norm(DN_ALPHA * x_ref[...] + mix, lg_ref[0:1, :], lb_ref[0:1, :])
    h_ref[...] = h
    hb_ref[...] = h.astype(jnp.bfloat16)


def _merge(br, gpre, x, wb, wo, lg, lb, tm):
    t, dm = x.shape
    kern = functools.partial(_merge_kernel, dm=dm)
    row = lambda i: (i, 0)
    return pl.pallas_call(
        kern,
        grid=(t // tm,),
        in_specs=[pl.BlockSpec((tm, BRANCH_DIM), row)] * 4 + [
            pl.BlockSpec((tm, N_BRANCHES * dm), row),
            pl.BlockSpec((tm, dm), row),
            _const_spec(wb.shape),
            _const_spec(wo.shape),
            _const_spec((8, dm)),
            _const_spec((8, dm)),
        ],
        out_specs=[pl.BlockSpec((tm, dm), row), pl.BlockSpec((tm, dm), row)],
        out_shape=[jax.ShapeDtypeStruct((t, dm), jnp.float32), jax.ShapeDtypeStruct((t, dm), jnp.bfloat16)],
        compiler_params=_cparams("parallel"),
        name="merge",
    )(*br, gpre, x, wb, wo, lg, lb)


def _top16(x):
    rank = jnp.full(x.shape, float(PEER_TOPK), jnp.float32)
    vals = []
    for k in range(PEER_TOPK):
        m = jnp.max(x, axis=0, keepdims=True)
        eq = x == m
        rank = jnp.where(eq, float(k), rank)
        x = jnp.where(eq, -jnp.inf, x)
        vals.append(m)
    return jnp.concatenate(vals, axis=0), rank


def _peerq_kernel(hb_ref, wq_ref, sk_ref, r2_ref, e2_ref, nq_ref, d_ref, q_scr):
    q_scr[...] = jnp.dot(hb_ref[...], wq_ref[...], preferred_element_type=jnp.float32).astype(jnp.bfloat16)
    k = PEER_TOPK
    for h in range(PEER_HEADS):
        sc = []
        for p in range(2):
            c0 = (2 * h + p) * PEER_KEY_DIM
            sc.append(lax.dot_general(sk_ref[p], q_scr[:, c0:c0 + PEER_KEY_DIM], (((1,), (1,)), ((), ())),
                                      preferred_element_type=jnp.float32))
        a, r1 = _top16(sc[0])
        b, r2 = _top16(sc[1])
        blocks = [a[0:1] + b]
        qrow = lax.broadcasted_iota(jnp.int32, (8, a.shape[1]), 0)
        for p in range(1, 8):
            blocks.append(jnp.where(qrow < k // (p + 1), a[p:p + 1] + b[0:8], -jnp.inf))
        blocks.append(a[8:16] + b[0:1])
        cand = jnp.concatenate(blocks, axis=0)
        best, _ = _top16(cand)
        top = best[0:1]
        sel = cand >= best[k - 1:k]
        z = jnp.sum(jnp.where(sel, jnp.exp(cand - top), 0.0), axis=0, keepdims=True)
        self32 = sel.astype(jnp.float32)
        n_rows = [jnp.sum(self32[0:16], axis=0, keepdims=True)]
        for p in range(1, 8):
            n_rows.append(jnp.sum(self32[8 + 8 * p:16 + 8 * p], axis=0, keepdims=True))
        nq = jnp.zeros(r1.shape, jnp.float32)
        for p in range(8):
            nq = jnp.where(r1 == float(p), n_rows[p], nq)
        tail = self32[72:80]
        for p in range(8, 16):
            nq = jnp.where(r1 == float(p), tail[p - 8:p - 7], nq)
        nq_ref[h] = nq
        d_ref[h] = jnp.exp(sc[0] - a[0:1]) / z
        r2_ref[h] = r2.astype(r2_ref.dtype)
        e2_ref[h] = jnp.exp(sc[1] - b[0:1]).astype(e2_ref.dtype)


def _peerq(hb, wq, sk, tm):
    t, dm = hb.shape
    shp = lambda dt: jax.ShapeDtypeStruct((PEER_HEADS, PEER_N_KEYS, t), dt)
    ospec = pl.BlockSpec((PEER_HEADS, PEER_N_KEYS, tm), lambda i: (0, 0, i))
    return pl.pallas_call(
        _peerq_kernel,
        grid=(t // tm,),
        in_specs=[pl.BlockSpec((tm, dm), lambda i: (i, 0)), _const_spec(wq.shape), _const_spec(sk.shape)],
        out_specs=[ospec] * 4,
        out_shape=[shp(jnp.bfloat16), shp(jnp.bfloat16), shp(jnp.float32), shp(jnp.float32)],
        scratch_shapes=[pltpu.VMEM((tm, wq.shape[1]), jnp.bfloat16)],
        compiler_params=_cparams("parallel"),
        name="peerq",
    )(hb, wq, sk)


def _row_bf16(ref, h, r):
    row = ref[h, r:r + 1, :]
    blk = jnp.broadcast_to(row, (16, row.shape[1])).astype(jnp.bfloat16)
    return jnp.concatenate([blk] * (PEER_N_KEYS // 16), axis=0)


def _peerd_kernel(hb_ref, u_ref, v_ref, r2_ref, e2_ref, nq_ref, d_ref, o_ref, a_scr, c_scr):
    j = pl.program_id(1)

    @pl.when(j == 0)
    def _():
        o_ref[...] = jnp.zeros_like(o_ref)

    n1 = u_ref.shape[0] // PEER_N_KEYS
    a_scr[...] = lax.dot_general(u_ref[...], hb_ref[...], (((1,), (1,)), ((), ())),
                                 preferred_element_type=jnp.float32)
    for r in range(n1):
        rows = slice(r * PEER_N_KEYS, (r + 1) * PEER_N_KEYS)
        w = None
        for h in range(PEER_HEADS):
            nqb = _row_bf16(nq_ref, h, r)
            db = _row_bf16(d_ref, h, r)
            term = jnp.where(r2_ref[h] < nqb, e2_ref[h], jnp.zeros_like(nqb)) * db
            w = term if w is None else w + term
        a = a_scr[rows, :]
        gelu = 0.5 * a * (1.0 + lax.erf(a * _SQRT_HALF))
        c_scr[rows, :] = w * gelu.astype(jnp.bfloat16)
    o_ref[...] += lax.dot_general(c_scr[...], v_ref[...], (((0,), (0,)), ((), ())),
                                  preferred_element_type=jnp.float32)


def _peerd(hb, ub, vb, r2, e2, nq, dd, tm, te):
    t, dm = hb.shape
    n1 = te // PEER_N_KEYS
    full = pl.BlockSpec((PEER_HEADS, PEER_N_KEYS, tm), lambda i, j: (0, 0, i))
    part = pl.BlockSpec((PEER_HEADS, n1, tm), lambda i, j: (0, j, i))
    return pl.pallas_call(
        _peerd_kernel,
        grid=(t // tm, ub.shape[0] // te),
        in_specs=[
            pl.BlockSpec((tm, dm), lambda i, j: (i, 0)),
            pl.BlockSpec((te, dm), lambda i, j: (j, 0)),
            pl.BlockSpec((te, dm), lambda i, j: (j, 0)),
            full, full, part, part,
        ],
        out_specs=pl.BlockSpec((tm, dm), lambda i, j: (i, 0)),
        out_shape=jax.ShapeDtypeStruct((t, dm), jnp.float32),
        scratch_shapes=[pltpu.VMEM((te, tm), jnp.float32), pltpu.VMEM((te, tm), jnp.bfloat16)],
        compiler_params=_cparams("parallel", "arbitrary"),
        name="peerd",
    )(hb, ub, vb, r2, e2, nq, dd)


def _ln2_kernel(h_ref, f_ref, g_ref, b_ref, x_ref, xb_ref):
    x = _layer_norm(DN_ALPHA * h_ref[...] + f_ref[...], g_ref[0:1, :], b_ref[0:1, :])
    x_ref[...] = x
    xb_ref[...] = x.astype(jnp.bfloat16)


def _ln2(h, ffn, g, b, tm):
    t, dm = h.shape
    row = lambda i: (i, 0)
    return pl.pallas_call(
        _ln2_kernel,
        grid=(t // tm,),
        in_specs=[pl.BlockSpec((tm, dm), row), pl.BlockSpec((tm, dm), row), _const_spec((8, dm)), _const_spec((8, dm))],
        out_specs=[pl.BlockSpec((tm, dm), row), pl.BlockSpec((tm, dm), row)],
        out_shape=[jax.ShapeDtypeStruct((t, dm), jnp.float32), jax.ShapeDtypeStruct((t, dm), jnp.bfloat16)],
        compiler_params=_cparams("parallel"),
        name="ln2",
    )(h, ffn, g, b)


def _tiles(t, seq):
    pick = lambda n, want: want if n % want == 0 else n
    return dict(
        mm_m=pick(t, 1024), mix_n=1280, gate_n=1024,
        convpool=pick(seq, 512), swa=pick(seq, 512), mlaprep=pick(seq, 512), mla_q=pick(seq, 512),
        merge=pick(t, 256), peerq=pick(t, 256), peerd_m=pick(t, 512), peerd_e=1024, ln2=pick(t, 512),
    )


def _row8(v):
    return jnp.broadcast_to(v.reshape(1, -1), (8, v.shape[-1]))


def kernel(x, w_in, conv_w, swa_sinks, mla_q_norm, mla_w_q_up, mla_kv_norm, mla_w_kv_up, pool_w, pool_scale,
           w_branch, w_out, ln1_g, ln1_b, peer_w_query, peer_sub_keys, peer_u, peer_v, ln2_g, ln2_b):
    batch, seq, dm = x.shape
    t = batch * seq
    depth = w_in.shape[0]
    tl = _tiles(t, seq)
    bf = jnp.bfloat16

    pos = jnp.arange(seq, dtype=jnp.float32)
    inv_freq = ROPE_THETA ** (-jnp.arange(0, MLA_ROPE_DIM, 2, dtype=jnp.float32) / MLA_ROPE_DIM)
    ang = pos[:, None] * inv_freq[None, :]
    cos, sin = jnp.cos(ang), jnp.sin(ang)
    cs = jnp.concatenate([cos, cos, sin, sin], axis=1)

    o_u, o_b, o_c, o_sq, o_sk, o_sv, o_cq, o_ckv, o_kr, o_pool, o_gate = np.cumsum(
        [0, 512, 512, 512, 512, 128, 128, 512, 256, 64, 512]).tolist()

    xf = x.reshape(t, dm)
    xb = xf.astype(bf)
    for l in range(depth):
        wi = w_in[l]
        w_kr = wi[:, o_kr:o_kr + MLA_ROPE_DIM]
        w_mix = jnp.concatenate([
            wi[:, o_u:o_sq],
            wi[:, o_sq:o_sq + 512],
            wi[:, o_cq:o_cq + 512],
            wi[:, o_pool:o_pool + 512],
            wi[:, o_ckv:o_ckv + 256],
            wi[:, o_sk:o_sk + 128],
            wi[:, o_sv:o_sv + 128],
            w_kr, _rot_half_cols(w_kr),
            jnp.zeros((dm, MIX_COLS - MIX_KR - 2 * MLA_ROPE_DIM), wi.dtype),
        ], axis=1).astype(bf)
        w_gate = wi[:, o_gate:].astype(bf)

        wq_up = mla_w_q_up[l].reshape(MLA_Q_RANK, MLA_HEADS, MLA_QK_DIM)
        wq_parts = []
        for h in range(MLA_HEADS):
            w_r = wq_up[:, h, MLA_NOPE_DIM:]
            wq_parts += [wq_up[:, h, :MLA_NOPE_DIM], w_r, _rot_half_cols(w_r)]
        wq = jnp.concatenate(wq_parts, axis=1).astype(bf)
        wkv_up = mla_w_kv_up[l].reshape(MLA_KV_RANK, MLA_HEADS, MLA_NOPE_DIM + MLA_V_DIM)
        wkv = jnp.concatenate([wkv_up[:, :, :MLA_NOPE_DIM].reshape(MLA_KV_RANK, -1),
                               wkv_up[:, :, MLA_NOPE_DIM:].reshape(MLA_KV_RANK, -1)], axis=1).astype(bf)

        hmix = _matmul(xb, w_mix, tl["mm_m"], tl["mix_n"], jnp.float32, "inproj_mix")
        gpre = _matmul(xb, w_gate, tl["mm_m"], tl["gate_n"], jnp.float32, "inproj_gate")

        br_a, br_d = _convpool(hmix, jnp.pad(conv_w[l], ((0, 5), (0, 0))), pool_w[l].astype(bf),
                               _row8(pool_scale[l]), batch, seq, tl["convpool"])
        br_b = _swa(hmix, swa_sinks[l], batch, seq, tl["swa"])
        qc, kc, vc = _mlaprep(hmix, cs, _row8(mla_q_norm[l]), _row8(mla_kv_norm[l]), wq, wkv, seq, tl["mlaprep"])
        br_c = _mla(qc, kc, vc, batch, seq, tl["mla_q"])

        h, hb = _merge((br_a, br_b, br_c, br_d), gpre, xf, w_branch[l].astype(bf), w_out[l].astype(bf),
                       _row8(ln1_g[l]), _row8(ln1_b[l]), tl["merge"])

        r2, e2, nq, dd = _peerq(hb, peer_w_query[l].astype(bf), peer_sub_keys[l].astype(bf), tl["peerq"])
        ffn = _peerd(hb, peer_u[l].astype(bf), peer_v[l].astype(bf), r2, e2, nq, dd, tl["peerd_m"], tl["peerd_e"])
        xf, xb = _ln2(h, ffn, _row8(ln2_g[l]), _row8(ln2_b[l]), tl["ln2"])
    return xf.reshape(batch, seq, dm)
```
